```python
import math
import jax
import jax.numpy as jnp
from jax import lax
import numpy as np

D_MODEL = 1024
BATCH = 16
SEQ = 2048
DEPTH = 4
DEC_BATCH = 8
DEC_SEQ = 16
PAST_LEN = 2048

CHUNK = 64
N_A = DEPTH // 2
N_B = DEPTH - N_A
HEAD_A = 64
H_A = D_MODEL // HEAD_A
LORA_DECAY = max(32, int(round(1.8 * math.sqrt(D_MODEL) / 32)) * 32)
LORA_A = max(32, int(round(1.8 * math.sqrt(D_MODEL) / 32)) * 32)
LORA_V = max(32, int(round(1.3 * math.sqrt(D_MODEL) / 32)) * 32)
LORA_G = max(32, int(round(0.6 * D_MODEL ** 0.8 / 32)) * 32)
LNX_EPS = 64e-5
H_B = 16
HEAD_B = D_MODEL // H_B
N_PAST_CHUNKS = 8
BAND_PAST = N_PAST_CHUNKS * CHUNK
REL_CLIP = 128
D_FF = -(-(8 * D_MODEL) // (3 * 256)) * 256
RMS_EPS = 1e-6
NEG_INF = -1e30

kernel_name = 'hybrid_rwkv7_chunkband_yoco_step'


def rms_norm(x, g):
    xf = x.astype(jnp.float32)
    y = xf * lax.rsqrt(jnp.mean(xf * xf, axis=-1, keepdims=True) + RMS_EPS)
    return (y * g.astype(jnp.float32)).astype(x.dtype)


def modulate(h, shift, scale):
    return h * (1.0 + scale[:, None, :]) + shift[:, None, :]


def to_heads(t, n_heads):
    return t.reshape(t.shape[0], t.shape[1], n_heads, -1)


def l2_normalize(t):
    tf = t.astype(jnp.float32)
    n = jnp.sqrt(jnp.sum(tf * tf, axis=-1, keepdims=True))
    return (tf / jnp.maximum(n, 1e-12)).astype(t.dtype)


def swiglu(h, w_gate, w_up, w_down):
    return (jax.nn.silu(h @ w_gate) * (h @ w_up)) @ w_down


def wkv7_scan(r, w, k, v, a, b, s0):
    def step(s, inp):
        r_t, w_t, k_t, v_t, a_t, b_t = inp
        sa = jnp.einsum('bhij,bhj->bhi', s, a_t)
        s = s * w_t[:, :, None, :] + sa[..., None] * b_t[:, :, None, :] + v_t[..., None] * k_t[:, :, None, :]
        y = jnp.einsum('bhij,bhj->bhi', s, r_t)
        return s, y
    xs = tuple(jnp.moveaxis(t.astype(jnp.float32), 1, 0) for t in (r, w, k, v, a, b))
    s_last, ys = lax.scan(step, s0.astype(jnp.float32), xs)
    return jnp.moveaxis(ys, 0, 1), s_last


def rwkv7_time_mix(h, shift_prev, wkv_prev, v_first, vres, mix, w_r, w_k, w_v, w_o, w0, w1, w2,
                   a0, a1, a2, g1, g2, k_k, k_a, r_k, ln_w, ln_b):
    bsz, t, d = h.shape
    h_prev = jnp.concatenate([shift_prev[:, None, :].astype(h.dtype), h[:, :-1]], axis=1)
    xx = h_prev - h
    xr = h + xx * mix[0]
    xw = h + xx * mix[1]
    xk = h + xx * mix[2]
    xv = h + xx * mix[3]
    xa = h + xx * mix[4]
    xg = h + xx * mix[5]
    r = xr @ w_r
    w_log = -jax.nn.softplus(-(w0 + jnp.tanh(xw @ w1) @ w2)) - 0.5
    decay = jnp.exp(-jnp.exp(w_log.astype(jnp.float32)))
    k = xk @ w_k
    v = xv @ w_v
    if vres is None:
        v_first = v
    else:
        v0, v1, v2 = vres
        v = v + (v_first - v) * jax.nn.sigmoid(v0 + (xv @ v1) @ v2)
    a_gate = jax.nn.sigmoid(a0 + (xa @ a1) @ a2)
    g = jax.nn.sigmoid(xg @ g1) @ g2
    kk = l2_normalize(to_heads(k * k_k, H_A))
    k = k * (1.0 + (a_gate - 1.0) * k_a)
    rh, kh, vh = to_heads(r, H_A), to_heads(k, H_A), to_heads(v, H_A)
    y, wkv_new = wkv7_scan(rh, to_heads(decay, H_A), kh, vh, -kk, kk * to_heads(a_gate, H_A), wkv_prev)
    mu = jnp.mean(y, axis=-1, keepdims=True)
    var = jnp.mean(jnp.square(y - mu), axis=-1, keepdims=True)
    y = ((y - mu) * lax.rsqrt(var + LNX_EPS)).reshape(bsz, t, d)
    y = y * ln_w.astype(jnp.float32) + ln_b.astype(jnp.float32)
    bonus = jnp.sum((rh * kh * r_k).astype(jnp.float32), axis=-1, keepdims=True) * vh.astype(jnp.float32)
    y = (y + bonus.reshape(bsz, t, d)).astype(h.dtype)
    out = (y * g) @ w_o
    return out, h[:, -1], wkv_new.astype(h.dtype), v_first


def shared_kv(x, c, kv_ada_w, kv_ada_b, kv_norm, w_kv, k_norm):
    bsz, t, _ = x.shape
    shift, scale = jnp.split(jax.nn.silu(c) @ kv_ada_w + kv_ada_b, 2, axis=-1)
    h = modulate(rms_norm(x, kv_norm), shift, scale)
    k, v = jnp.split(h @ w_kv, 2, axis=-1)
    k = rms_norm(k.reshape(bsz, t, H_B, HEAD_B), k_norm)
    v = v.reshape(bsz, t, H_B, HEAD_B)
    return k, v


def band_attention(q, k_all, v_all, pos0, hist_len, rel_table, qb):
    bsz, t, nh, dh = q.shape
    n_blk = t // qb
    band = hist_len + qb
    scale = dh ** -0.5

    def block(i):
        start = i * qb
        q_b = lax.dynamic_slice_in_dim(q, start, qb, axis=1)
        k_b = lax.dynamic_slice_in_dim(k_all, start, band, axis=1)
        v_b = lax.dynamic_slice_in_dim(v_all, start, band, axis=1)
        q_pos = pos0 + start + jnp.arange(qb)
        k_pos = pos0 + start - hist_len + jnp.arange(band)
        rel = jnp.clip(q_pos[:, None] - k_pos[None, :], -REL_CLIP, REL_CLIP) + REL_CLIP
        bias = rel_table[:, rel].astype(jnp.float32)
        q_chunk = q_pos // CHUNK
        k_chunk = k_pos // CHUNK
        allowed = ((k_pos[None, :] >= 0) & (k_chunk[None, :] <= q_chunk[:, None])
                   & (k_chunk[None, :] >= q_chunk[:, None] - N_PAST_CHUNKS))
        s = jnp.einsum('bqhd,bkhd->bhqk', q_b, k_b).astype(jnp.float32) * scale + bias[None]
        s = jnp.where(allowed[None, None], s, NEG_INF)
        p = jax.nn.softmax(s, axis=-1)
        return jnp.einsum('bhqk,bkhd->bqhd', p.astype(v_b.dtype), v_b)

    out = lax.map(block, jnp.arange(n_blk))
    return jnp.moveaxis(out, 0, 1).reshape(bsz, t, nh, dh)


def run_trunk(x, c, shift0, wkv0, k_hist, v_hist, pos0, p):
    bsz, t, d = x.shape
    qb = min(t, CHUNK)
    v_first = None
    shifts = []
    wkvs = []
    k_new = None
    v_new = None
    k_all = None
    v_all = None
    for l in range(DEPTH):
        mod = jax.nn.silu(c) @ p['w_ada'][l] + p['b_ada'][l]
        sh_m, sc_m, gt_m, sh_f, sc_f, gt_f = jnp.split(mod, 6, axis=-1)
        h = modulate(rms_norm(x, p['norm_mix'][l]), sh_m, sc_m)
        if l < N_A:
            vres = None if l == 0 else (p['rw_v0'][l - 1], p['rw_v1'][l - 1], p['rw_v2'][l - 1])
            out, sh, st, v_first = rwkv7_time_mix(
                h, shift0[l], wkv0[l], v_first, vres, p['rw_mix'][l], p['rw_r'][l], p['rw_k'][l],
                p['rw_v'][l], p['rw_o'][l], p['rw_w0'][l], p['rw_w1'][l], p['rw_w2'][l],
                p['rw_a0'][l], p['rw_a1'][l], p['rw_a2'][l], p['rw_g1'][l], p['rw_g2'][l],
                p['rw_kk'][l], p['rw_ka'][l], p['rw_rk'][l], p['rw_lnw'][l], p['rw_lnb'][l])
            shifts.append(sh)
            wkvs.append(st)
        else:
            if l == N_A:
                k_new, v_new = shared_kv(x, c, p['kv_ada_w'], p['kv_ada_b'], p['kv_norm'],
                                         p['w_kv'], p['k_norm'])
                k_all = jnp.concatenate([k_hist.astype(k_new.dtype), k_new], axis=1)
                v_all = jnp.concatenate([v_hist.astype(v_new.dtype), v_new], axis=1)
            j = l - N_A
            q = rms_norm((h @ p['wb_q'][j]).reshape(bsz, t, H_B, HEAD_B), p['q_norm'][j])
            o = band_attention(q, k_all, v_all, pos0, k_hist.shape[1], p['rel_bias'][j], qb)
            out = o.reshape(bsz, t, d) @ p['wb_o'][j]
        x = x + gt_m[:, None, :] * out
        h = modulate(rms_norm(x, p['norm_ffn'][l]), sh_f, sc_f)
        x = x + gt_f[:, None, :] * swiglu(h, p['w_gate'][l], p['w_up'][l], p['w_down'][l])
    return x, jnp.stack(shifts), jnp.stack(wkvs), k_new, v_new


def setup_inputs(seed: int = 0) -> dict:
    key = jax.random.key(seed)
    ks = iter(jax.random.split(key, 64))
    f32 = jnp.float32

    def nrm(shape, scale):
        return jax.random.normal(next(ks), shape, f32) * scale

    def gain(shape):
        return 1.0 + nrm(shape, 0.02)

    def unif(shape, lo, hi):
        return jax.random.uniform(next(ks), shape, f32, lo, hi)

    d = D_MODEL
    inv = d ** -0.5
    rows = min(BAND_PAST, PAST_LEN)
    return {
        'x_prompt': nrm((BATCH, SEQ, d), 1.0),
        'x_sample': nrm((DEC_BATCH, DEC_SEQ, d), 1.0),
        'c_prompt': nrm((BATCH, d), 1.0),
        'c_sample': nrm((DEC_BATCH, d), 1.0),
        'state_shift': nrm((N_A, DEC_BATCH, d), 1.0),
        'state_wkv': nrm((N_A, DEC_BATCH, H_A, HEAD_A, HEAD_A), 0.5),
        'cache_k': nrm((DEC_BATCH, rows, H_B, HEAD_B), 1.0),
        'cache_v': nrm((DEC_BATCH, rows, H_B, HEAD_B), 1.0),
        'w_ada': nrm((DEPTH, d, 6 * d), 0.5 * inv),
        'b_ada': nrm((DEPTH, 6 * d), 0.02),
        'norm_mix': gain((DEPTH, d)),
        'norm_ffn': gain((DEPTH, d)),
        'rw_mix': unif((N_A, 6, d), 0.0, 1.0),
        'rw_r': nrm((N_A, d, d), inv),
        'rw_k': nrm((N_A, d, d), inv),
        'rw_v': nrm((N_A, d, d), inv),
        'rw_o': nrm((N_A, d, d), inv),
        'rw_w0': unif((N_A, d), -6.5, -1.5),
        'rw_w1': nrm((N_A, d, LORA_DECAY), inv),
        'rw_w2': nrm((N_A, LORA_DECAY, d), 0.1 * LORA_DECAY ** -0.5),
        'rw_a0': nrm((N_A, d), 0.1),
        'rw_a1': nrm((N_A, d, LORA_A), inv),
        'rw_a2': nrm((N_A, LORA_A, d), 0.1 * LORA_A ** -0.5),
        'rw_v0': 1.0 + nrm((N_A - 1, d), 0.1),
        'rw_v1': nrm((N_A - 1, d, LORA_V), inv),
        'rw_v2': nrm((N_A - 1, LORA_V, d), 0.1 * LORA_V ** -0.5),
        'rw_g1': nrm((N_A, d, LORA_G), inv),
        'rw_g2': nrm((N_A, LORA_G, d), LORA_G ** -0.5),
        'rw_kk': 0.85 + nrm((N_A, d), 0.05),
        'rw_ka': 1.0 + nrm((N_A, d), 0.05),
        'rw_rk': -0.04 + nrm((N_A, H_A, HEAD_A), 0.02),
        'rw_lnw': gain((N_A, d)),
        'rw_lnb': nrm((N_A, d), 0.02),
        'kv_ada_w': nrm((d, 2 * d), 0.5 * inv),
        'kv_ada_b': nrm((2 * d,), 0.02),
        'kv_norm': gain((d,)),
        'w_kv': nrm((d, 2 * d), inv),
        'k_norm': gain((HEAD_B,)),
        'wb_q': nrm((N_B, d, d), inv),
        'q_norm': gain((N_B, HEAD_B)),
        'rel_bias': nrm((N_B, H_B, 2 * REL_CLIP + 1), 0.5),
        'wb_o': nrm((N_B, d, d), inv),
        'w_gate': nrm((DEPTH, d, D_FF), inv),
        'w_up': nrm((DEPTH, d, D_FF), inv),
        'w_down': nrm((DEPTH, D_FF, d), D_FF ** -0.5),
    }


def reference(x_prompt, x_sample, c_prompt, c_sample, state_shift, state_wkv, cache_k, cache_v,
              w_ada, b_ada, norm_mix, norm_ffn, rw_mix, rw_r, rw_k, rw_v, rw_o, rw_w0, rw_w1, rw_w2,
              rw_a0, rw_a1, rw_a2, rw_v0, rw_v1, rw_v2, rw_g1, rw_g2, rw_kk, rw_ka, rw_rk, rw_lnw,
              rw_lnb, kv_ada_w, kv_ada_b, kv_norm, w_kv, k_norm, wb_q, q_norm, rel_bias, wb_o,
              w_gate, w_up, w_down):
    p = dict(w_ada=w_ada, b_ada=b_ada, norm_mix=norm_mix, norm_ffn=norm_ffn, rw_mix=rw_mix,
             rw_r=rw_r, rw_k=rw_k, rw_v=rw_v, rw_o=rw_o, rw_w0=rw_w0, rw_w1=rw_w1, rw_w2=rw_w2,
             rw_a0=rw_a0, rw_a1=rw_a1, rw_a2=rw_a2, rw_v0=rw_v0, rw_v1=rw_v1, rw_v2=rw_v2,
             rw_g1=rw_g1, rw_g2=rw_g2, rw_kk=rw_kk, rw_ka=rw_ka, rw_rk=rw_rk, rw_lnw=rw_lnw,
             rw_lnb=rw_lnb, kv_ada_w=kv_ada_w, kv_ada_b=kv_ada_b, kv_norm=kv_norm, w_kv=w_kv,
             k_norm=k_norm, wb_q=wb_q, q_norm=q_norm, rel_bias=rel_bias, wb_o=wb_o,
             w_gate=w_gate, w_up=w_up, w_down=w_down)
    b_p = x_prompt.shape[0]
    t_p = x_prompt.shape[1]
    shift0_p = jnp.zeros((N_A, b_p, D_MODEL), x_prompt.dtype)
    wkv0_p = jnp.zeros((N_A, b_p, H_A, HEAD_A, HEAD_A), x_prompt.dtype)
    hist0_p = jnp.zeros((b_p, BAND_PAST, H_B, HEAD_B), x_prompt.dtype)
    y_prompt, shift_p, wkv_p, k_p, v_p = run_trunk(x_prompt, c_prompt, shift0_p, wkv0_p,
                                                   hist0_p, hist0_p, 0, p)
    keep = min(BAND_PAST, t_p)
    k_p = k_p[:, t_p - keep:]
    v_p = v_p[:, t_p - keep:]
    y_sample, shift_s, wkv_s, k_s, v_s = run_trunk(x_sample, c_sample, state_shift, state_wkv,
                                                   cache_k, cache_v, PAST_LEN, p)
    return (y_prompt, y_sample, shift_p, wkv_p, k_p, v_p, shift_s, wkv_s, k_s, v_s)
```

```python
import functools
import math

import jax
import jax.numpy as jnp
from jax import lax
from jax.experimental import pallas as pl
from jax.experimental.pallas import tpu as pltpu

F32 = jnp.float32
BF16 = jnp.bfloat16

D_MODEL = 1024
DEPTH = 4
N_A = DEPTH // 2
N_B = DEPTH - N_A
HEAD = 64
N_HEADS = D_MODEL // HEAD
CHUNK = 64
N_PAST_CHUNKS = 8
BAND_PAST = N_PAST_CHUNKS * CHUNK
REL_CLIP = 128
D_FF = -(-(8 * D_MODEL) // (3 * 256)) * 256
RMS_EPS = 1e-6
LNX_EPS = 64e-5
NEG_INF = -1e30

LANES = 128
SUBLANES = 8
MXU_DIM = 256

WKV_CHUNK = 64
WKV_GROUP = MXU_DIM
HEADS_PER_GROUP = WKV_GROUP // HEAD
N_GROUPS = D_MODEL // WKV_GROUP
ATT_Q = 2 * CHUNK
ATT_KEYS = BAND_PAST + ATT_Q
BIAS_W = ATT_KEYS + ATT_Q
VMEM_LIMIT = 56 * 1024 * 1024


def _cparams(sem):
    return pltpu.CompilerParams(dimension_semantics=sem, vmem_limit_bytes=VMEM_LIMIT)


def _mm(a, b, dims=((1,), (0,))):
    return lax.dot_general(a.astype(BF16), b.astype(BF16), (dims, ((), ())),
                           preferred_element_type=F32)


def _mm_exact(a, b, dims=((1,), (0,))):
    return lax.dot_general(a.astype(F32), b.astype(F32), (dims, ((), ())),
                           precision=lax.Precision.HIGHEST, preferred_element_type=F32)


def _modnorm(x, g, shift, scale):
    ms = jnp.mean(x * x, axis=-1, keepdims=True)
    return x * lax.rsqrt(ms + RMS_EPS) * g * (1.0 + scale) + shift


def _sigmoid(z):
    return 1.0 / (1.0 + jnp.exp(-z))


def _head_sum(z, sel):
    return _mm(z, sel)


def _head_bcast(s, sel_t2):
    hi = s.astype(BF16)
    lo = (s - hi.astype(F32)).astype(BF16)
    return lax.dot_general(jnp.concatenate([hi, lo], axis=1), sel_t2,
                           (((1,), (0,)), ((), ())), preferred_element_type=F32)


def _ada_kernel(c_ref, w_ref, b_ref, o_ref):
    c = c_ref[...]
    s = c * _sigmoid(c)
    o_ref[0] = _mm_exact(s, w_ref[0]) + b_ref[0]


def _ada(c_all, w, b):
    n_l, d, n = w.shape
    bc = c_all.shape[0]
    tn = 1536 if n % 1536 == 0 else n
    return pl.pallas_call(
        _ada_kernel,
        out_shape=jax.ShapeDtypeStruct((n_l, bc, n), F32),
        grid=(n_l, n // tn),
        in_specs=[pl.BlockSpec((bc, d), lambda l, j: (0, 0)),
                  pl.BlockSpec((1, d, tn), lambda l, j: (l, 0, j)),
                  pl.BlockSpec((1, 1, tn), lambda l, j: (l, 0, j))],
        out_specs=pl.BlockSpec((1, bc, tn), lambda l, j: (l, 0, j)),
        compiler_params=_cparams(("parallel", "parallel")),
        name="ada",
    )(c_all, w, b.reshape(n_l, 1, n))


def _tok_spec(tm, d=D_MODEL):
    return pl.BlockSpec((1, tm, d), lambda b, j: (b, j, 0))


def _mod_spec(arr, tm):
    if arr.shape[1] == 1:
        return pl.BlockSpec((1, 1, arr.shape[2]), lambda b, j: (b, 0, 0))
    return pl.BlockSpec((1, tm, arr.shape[2]), lambda b, j: (b, j, 0))


def _const_spec(arr):
    nd = arr.ndim
    return pl.BlockSpec(arr.shape, lambda b, j, _n=nd: (0,) * _n)


def _rwkv_pre_kernel(has_vres, *refs):
    (x_ref, xp_ref, sp_ref, sh_ref, sc_ref, g_ref, mix_ref, wr_ref, wk_ref, wv_ref,
     w0_ref, w1_ref, w2_ref, a0_ref, a1_ref, a2_ref, g1_ref, g2_ref, kkw_ref, kaw_ref,
     sel_ref, selt_ref) = refs[:22]
    refs = refs[22:]
    if has_vres:
        v0_ref, v1_ref, v2_ref, vf_ref = refs[:4]
        refs = refs[4:]
    r_out, lw_out, k_out, v_out, kk_out, b_out, g_out, shift_out = refs

    j = pl.program_id(1)
    x = x_ref[0]
    g = g_ref[...]
    sh = sh_ref[0]
    sc = sc_ref[0]
    h = _modnorm(x, g, sh, sc)
    hp_row = _modnorm(xp_ref[0][SUBLANES - 1:SUBLANES, :], g, sh, sc)
    prev_row = jnp.where(j == 0, sp_ref[0], hp_row)
    row = lax.broadcasted_iota(jnp.int32, h.shape, 0)
    h_prev = jnp.where(row == 0, prev_row, pltpu.roll(h, 1, 0))
    xx = h_prev - h
    mix = mix_ref[...]
    xr = h + xx * mix[0:1]
    xw = h + xx * mix[1:2]
    xk = h + xx * mix[2:3]
    xv = h + xx * mix[3:4]
    xa = h + xx * mix[4:5]
    xg = h + xx * mix[5:6]

    r = _mm(xr, wr_ref[...])
    k = _mm(xk, wk_ref[...])
    v = _mm(xv, wv_ref[...])
    z = w0_ref[...] + _mm(jnp.tanh(_mm(xw, w1_ref[...])), w2_ref[...])
    softplus_neg = jnp.maximum(-z, 0.0) + jnp.log(1.0 + jnp.exp(-jnp.abs(z)))
    lw_out[0] = -jnp.exp(-softplus_neg - 0.5)
    a_gate = _sigmoid(a0_ref[...] + _mm(_mm(xa, a1_ref[...]), a2_ref[...]))
    g_out[0] = _mm(_sigmoid(_mm(xg, g1_ref[...])), g2_ref[...])
    if has_vres:
        gate_v = _sigmoid(v0_ref[...] + _mm(_mm(xv, v1_ref[...]), v2_ref[...]))
        v = v + (vf_ref[0] - v) * gate_v
    kkr = k * kkw_ref[...]
    ss = _head_sum(kkr * kkr, sel_ref[...])
    inv = 1.0 / jnp.maximum(jnp.sqrt(ss), 1e-12)
    kk = kkr * _head_bcast(inv, selt_ref[...])
    r_out[0] = r
    k_out[0] = k * (1.0 + (a_gate - 1.0) * kaw_ref[...])
    v_out[0] = v
    kk_out[0] = kk
    b_out[0] = kk * a_gate
    shift_out[0] = h[h.shape[0] - 1:, :]


def _rwkv_pre(x, shift_prev, sh, sc, consts, vres, tm):
    bsz, t, d = x.shape
    has_vres = vres is not None
    nblk8 = tm // SUBLANES
    args = [x, x, shift_prev, sh, sc] + list(consts)
    specs = [_tok_spec(tm),
             pl.BlockSpec((1, SUBLANES, d), lambda b, j: (b, jnp.maximum(j * nblk8 - 1, 0), 0)),
             pl.BlockSpec((1, 1, d), lambda b, j: (b, 0, 0)),
             _mod_spec(sh, tm), _mod_spec(sc, tm)] + [_const_spec(c) for c in consts]
    if has_vres:
        v0, v1, v2, vf = vres
        args += [v0, v1, v2, vf]
        specs += [_const_spec(v0), _const_spec(v1), _const_spec(v2), _tok_spec(tm)]
    tok = jax.ShapeDtypeStruct((bsz, t, d), F32)
    return pl.pallas_call(
        functools.partial(_rwkv_pre_kernel, has_vres),
        out_shape=[tok] * 7 + [jax.ShapeDtypeStruct((bsz, 1, d), F32)],
        grid=(bsz, t // tm),
        in_specs=specs,
        out_specs=[_tok_spec(tm)] * 7 + [pl.BlockSpec((1, 1, d), lambda b, j: (b, 0, 0))],
        compiler_params=_cparams(("parallel", "arbitrary")),
        name="rwkv_pre",
    )(*args)


def _wkv_kernel(r_ref, lw_ref, k_ref, v_ref, kk_ref, b_ref, s0_ref, ltri_ref, tile_ref,
                y_ref, sT_ref, s_scr):
    c = pl.program_id(1)
    n_c = pl.num_programs(1)
    L = WKV_CHUNK
    W = WKV_GROUP
    r_i = lax.broadcasted_iota(jnp.int32, (W, W), 0)
    c_i = lax.broadcasted_iota(jnp.int32, (W, W), 1)
    bd = (r_i // HEAD) == (c_i // HEAD)
    t_i = lax.broadcasted_iota(jnp.int32, (L, W), 0)
    s_i = lax.broadcasted_iota(jnp.int32, (L, W), 1) % HEAD
    strict = s_i < t_i
    incl = s_i <= t_i
    eye_row = (s_i == t_i).astype(F32)

    def stack(m):
        mb = m.astype(BF16)
        return jnp.where(bd, jnp.concatenate([mb] * HEADS_PER_GROUP, axis=0), jnp.zeros((), BF16))

    @pl.when(c == 0)
    def _():
        for g in range(N_GROUPS):
            s0 = s0_ref[0, g * W:(g + 1) * W, :]
            s_scr[g] = jnp.where(bd, _mm_exact(s0, tile_ref[...]), 0.0)

    ltri = ltri_ref[...]
    for g in range(N_GROUPS):
        cs = slice(g * W, (g + 1) * W)
        lw = lw_ref[0, :, cs]
        cum = _mm_exact(ltri, lw)
        p = jnp.exp(cum)
        pinv = jnp.exp(-cum)
        pend = jnp.exp(cum[L - 1:L, :])
        kk = kk_ref[0, :, cs]
        at = -kk * jnp.exp(cum - lw)
        rt = r_ref[0, :, cs] * p
        bt = b_ref[0, :, cs] * pinv
        kt = k_ref[0, :, cs] * pinv
        v = v_ref[0, :, cs]
        s_old = s_scr[g]

        gram = _mm(jnp.concatenate([at, rt], axis=0),
                   jnp.concatenate([stack(bt), stack(kt)], axis=0), ((1,), (1,)))
        m_ab = jnp.where(strict, gram[:L, :W], 0.0)
        m_ak = jnp.where(strict, gram[:L, W:], 0.0)
        q_b = jnp.where(incl, gram[L:, :W], 0.0)
        q_k = jnp.where(incl, gram[L:, W:], 0.0)

        pw = m_ab
        tm = eye_row + m_ab
        for _ in range(int(math.log2(L)) - 1):
            pw = _mm(pw, stack(pw))
            tm = tm + _mm(tm, stack(pw))

        v_bd = stack(v)
        s_bf = s_old.astype(BF16)
        xs = _mm(at, s_bf, ((1,), (1,))) + _mm(m_ak, v_bd)
        u = _mm(tm, stack(xs))
        y = _mm(rt, s_bf, ((1,), (1,))) + _mm(q_b, stack(u)) + _mm(q_k, v_bd)
        y_ref[0, :, cs] = y
        upd = _mm(u, bt * pend, ((0,), (0,))) + _mm(v, kt * pend, ((0,), (0,)))
        s_new = s_old * pend + jnp.where(bd, upd, 0.0)
        s_scr[g] = s_new

        @pl.when(c == n_c - 1)
        def _():
            sT_ref[0, g * W:(g + 1) * W, :] = _mm_exact(s_new, tile_ref[...], ((1,), (1,)))


def _wkv(r, lw, k, v, kk, b, s0):
    bsz, t, d = r.shape
    L = WKV_CHUNK
    ltri = (jnp.arange(L)[:, None] >= jnp.arange(L)[None, :]).astype(F32)
    tile = (jnp.arange(HEAD)[:, None] == (jnp.arange(WKV_GROUP)[None, :] % HEAD)).astype(F32)
    spec = pl.BlockSpec((1, L, d), lambda bb, c: (bb, c, 0))
    sspec = pl.BlockSpec((1, d, HEAD), lambda bb, c: (bb, 0, 0))
    return pl.pallas_call(
        _wkv_kernel,
        out_shape=[jax.ShapeDtypeStruct((bsz, t, d), F32),
                   jax.ShapeDtypeStruct((bsz, d, HEAD), F32)],
        grid=(bsz, t // L),
        in_specs=[spec] * 6 + [sspec, _const_spec(ltri), _const_spec(tile)],
        out_specs=[spec, sspec],
        scratch_shapes=[pltpu.VMEM((N_GROUPS, WKV_GROUP, WKV_GROUP), F32)],
        compiler_params=_cparams(("parallel", "arbitrary")),
        name="wkv",
    )(r, lw, k, v, kk, b, s0, ltri, tile)


def _rwkv_post_kernel(y_ref, r_ref, k_ref, v_ref, g_ref, x_ref, gate_ref, lnw_ref, lnb_ref,
                      rk_ref, wo_ref, sel_ref, selt_ref, o_ref):
    y = y_ref[0]
    sel = sel_ref[...]
    selt = selt_ref[...]
    mu = _head_bcast(_head_sum(y, sel) * (1.0 / HEAD), selt)
    dlt = y - mu
    var = _head_sum(dlt * dlt, sel) * (1.0 / HEAD)
    yn = dlt * _head_bcast(lax.rsqrt(var + LNX_EPS), selt) * lnw_ref[...] + lnb_ref[...]
    v = v_ref[0]
    bonus = _head_bcast(_head_sum(r_ref[0] * k_ref[0] * rk_ref[...], sel), selt) * v
    z = (yn + bonus) * g_ref[0]
    o_ref[0] = x_ref[0] + gate_ref[0] * _mm(z, wo_ref[...])


def _rwkv_post(y, r, k, v, g, x, gate, consts, tm):
    bsz, t, d = x.shape
    return pl.pallas_call(
        _rwkv_post_kernel,
        out_shape=jax.ShapeDtypeStruct((bsz, t, d), F32),
        grid=(bsz, t // tm),
        in_specs=[_tok_spec(tm)] * 6 + [_mod_spec(gate, tm)] + [_const_spec(c) for c in consts],
        out_specs=_tok_spec(tm),
        compiler_params=_cparams(("parallel", "parallel")),
        name="rwkv_post",
    )(y, r, k, v, g, x, gate, *consts)


def _ffn_kernel(x_ref, sh_ref, sc_ref, gate_ref, g_ref, wg_ref, wu_ref, wd_ref, o_ref,
                h_scr, acc_scr):
    f = pl.program_id(2)

    @pl.when(f == 0)
    def _():
        h_scr[...] = _modnorm(x_ref[0], g_ref[...], sh_ref[0], sc_ref[0]).astype(BF16)
        acc_scr[...] = jnp.zeros_like(acc_scr)

    h = h_scr[...]
    a = _mm(h, wg_ref[...])
    u = _mm(h, wu_ref[...])
    acc_scr[...] += _mm(a * _sigmoid(a) * u, wd_ref[...])

    @pl.when(f == pl.num_programs(2) - 1)
    def _():
        o_ref[0] = x_ref[0] + gate_ref[0] * acc_scr[...]


def _ffn(x, sh, sc, gate, g, wg, wu, wd, tm, tf):
    bsz, t, d = x.shape
    dff = wg.shape[1]

    def mod_spec(arr):
        if arr.shape[1] == 1:
            return pl.BlockSpec((1, 1, d), lambda b, j, f: (b, 0, 0))
        return pl.BlockSpec((1, tm, d), lambda b, j, f: (b, j, 0))

    tok = pl.BlockSpec((1, tm, d), lambda b, j, f: (b, j, 0))
    return pl.pallas_call(
        _ffn_kernel,
        out_shape=jax.ShapeDtypeStruct((bsz, t, d), F32),
        grid=(bsz, t // tm, dff // tf),
        in_specs=[tok, mod_spec(sh), mod_spec(sc), mod_spec(gate),
                  pl.BlockSpec((1, d), lambda b, j, f: (0, 0)),
                  pl.BlockSpec((d, tf), lambda b, j, f: (0, f)),
                  pl.BlockSpec((d, tf), lambda b, j, f: (0, f)),
                  pl.BlockSpec((tf, d), lambda b, j, f: (f, 0))],
        out_specs=tok,
        scratch_shapes=[pltpu.VMEM((tm, d), BF16), pltpu.VMEM((tm, d), F32)],
        compiler_params=_cparams(("parallel", "parallel", "arbitrary")),
        name="ffn",
    )(x, sh, sc, gate, g, wg, wu, wd)


def _head_rms(z, w_tiled, sel, selt):
    ms = _head_sum(z * z, sel) * (1.0 / HEAD)
    return z * _head_bcast(lax.rsqrt(ms + RMS_EPS), selt) * w_tiled


def _attn_q_kernel(x_ref, sh_ref, sc_ref, g_ref, wq_ref, qn_ref, sel_ref, selt_ref, o_ref):
    h = _modnorm(x_ref[0], g_ref[...], sh_ref[0], sc_ref[0])
    o_ref[0] = _head_rms(_mm(h, wq_ref[...]), qn_ref[...], sel_ref[...], selt_ref[...])


def _attn_q(x, sh, sc, g, wq, qn, sel, selt, tm):
    bsz, t, d = x.shape
    consts = [g, wq, qn, sel, selt]
    return pl.pallas_call(
        _attn_q_kernel,
        out_shape=jax.ShapeDtypeStruct((bsz, t, d), F32),
        grid=(bsz, t // tm),
        in_specs=[_tok_spec(tm), _mod_spec(sh, tm), _mod_spec(sc, tm)]
        + [_const_spec(c) for c in consts],
        out_specs=_tok_spec(tm),
        compiler_params=_cparams(("parallel", "parallel")),
        name="attn_q",
    )(x, sh, sc, *consts)


def _attn_kv_kernel(x_ref, sh_ref, sc_ref, g_ref, wkv_ref, kn_ref, sel_ref, selt_ref,
                    k_ref, v_ref):
    h = _modnorm(x_ref[0], g_ref[...], sh_ref[0], sc_ref[0])
    kv = _mm(h, wkv_ref[...])
    k_ref[0] = _head_rms(kv[:, :D_MODEL], kn_ref[...], sel_ref[...], selt_ref[...])
    v_ref[0] = kv[:, D_MODEL:]


def _attn_kv(x, sh, sc, g, wkv, kn, sel, selt, tm):
    bsz, t, d = x.shape
    consts = [g, wkv, kn, sel, selt]
    tok = jax.ShapeDtypeStruct((bsz, t, d), F32)
    return pl.pallas_call(
        _attn_kv_kernel,
        out_shape=[tok, tok],
        grid=(bsz, t // tm),
        in_specs=[_tok_spec(tm), _mod_spec(sh, tm), _mod_spec(sc, tm)]
        + [_const_spec(c) for c in consts],
        out_specs=[_tok_spec(tm), _tok_spec(tm)],
        compiler_params=_cparams(("parallel", "parallel")),
        name="attn_kv",
    )(x, sh, sc, *consts)


def _attn_out_kernel(o_ref, x_ref, gate_ref, wo_ref, out_ref):
    out_ref[0] = x_ref[0] + gate_ref[0] * _mm(o_ref[0], wo_ref[...])


def _attn_out(o, x, gate, wo, tm):
    bsz, t, d = x.shape
    return pl.pallas_call(
        _attn_out_kernel,
        out_shape=jax.ShapeDtypeStruct((bsz, t, d), F32),
        grid=(bsz, t // tm),
        in_specs=[_tok_spec(tm), _tok_spec(tm), _mod_spec(gate, tm), _const_spec(wo)],
        out_specs=_tok_spec(tm),
        compiler_params=_cparams(("parallel", "parallel")),
        name="attn_out",
    )(o, x, gate, wo)


def _band_attn_kernel(prompt, n_q, n_kv, tab_ref, q_ref, k_ref, v_ref, o_ref,
                      kpad, vpad, bias_scr):
    pair = pl.program_id(1)
    rows_q = ATT_Q if prompt else n_q
    i_row = lax.broadcasted_iota(jnp.int32, (SUBLANES, BIAS_W), 1)
    for hh in range(2):
        head = pair * 2 + hh
        base = jnp.full((SUBLANES, BIAS_W), tab_ref[head, 2 * REL_CLIP], F32)

        def body(rr, acc, head=head):
            return jnp.where(i_row == BIAS_W - rr, tab_ref[head, rr], acc)

        base = lax.fori_loop(0, 2 * REL_CLIP, body, base)
        full = jnp.broadcast_to(base[0:1, :], (rows_q, BIAS_W))
        rolled = pltpu.roll(full, BIAS_W - ATT_Q, 1, stride=1, stride_axis=0)
        bias = rolled[:, :ATT_KEYS]
        qi = lax.broadcasted_iota(jnp.int32, (rows_q, ATT_KEYS), 0)
        ji = lax.broadcasted_iota(jnp.int32, (rows_q, ATT_KEYS), 1)
        if prompt:
            kc = ji // CHUNK - N_PAST_CHUNKS
            qc = qi // CHUNK
            ok = (kc <= qc) & (kc >= qc - N_PAST_CHUNKS)
        else:
            ok = ji < n_kv
        bias_scr[hh] = jnp.where(ok, bias, NEG_INF)

    if prompt:
        kpad[:BAND_PAST, :] = jnp.zeros((BAND_PAST, LANES), BF16)
        vpad[:BAND_PAST, :] = jnp.zeros((BAND_PAST, LANES), BF16)
        kpad[BAND_PAST:, :] = k_ref[0].astype(BF16)
        vpad[BAND_PAST:, :] = v_ref[0].astype(BF16)
    else:
        pad = ATT_KEYS - n_kv
        kpad[:n_kv, :] = k_ref[0].astype(BF16)
        vpad[:n_kv, :] = v_ref[0].astype(BF16)
        kpad[n_kv:, :] = jnp.zeros((pad, LANES), BF16)
        vpad[n_kv:, :] = jnp.zeros((pad, LANES), BF16)

    lane = lax.broadcasted_iota(jnp.int32, (rows_q, LANES), 1)
    lo = lane < HEAD
    scale = HEAD ** -0.5

    def step(s, carry):
        q0 = pl.multiple_of(s * rows_q, rows_q)
        q2 = q_ref[0, pl.ds(q0, rows_q), :] * scale
        kb = kpad[pl.ds(q0, ATT_KEYS), :]
        vb = vpad[pl.ds(q0, ATT_KEYS), :]
        outs = []
        for hh in range(2):
            qh = jnp.where(lo if hh == 0 else ~lo, q2, 0.0)
            sc = _mm(qh, kb, ((1,), (1,))) + bias_scr[hh]
            if prompt:
                jj = lax.broadcasted_iota(jnp.int32, (rows_q, ATT_KEYS), 1)
                sc = jnp.where(jj >= BAND_PAST - q0, sc, NEG_INF)
            m = jnp.max(sc, axis=-1, keepdims=True)
            p = jnp.exp(sc - m)
            l = jnp.sum(p, axis=-1, keepdims=True)
            outs.append(_mm(p, vb) / l)
        o_ref[0, pl.ds(q0, rows_q), :] = jnp.where(lo, outs[0], outs[1])
        return carry

    lax.fori_loop(0, n_q // rows_q, step, 0)


def _band_attn(q, k_all, v_all, rel_table, prompt):
    bsz, n_q, d = q.shape
    n_kv = k_all.shape[1]
    rows_pad = BAND_PAST + n_q if prompt else ATT_KEYS
    rows_q = ATT_Q if prompt else n_q
    return pl.pallas_call(
        functools.partial(_band_attn_kernel, prompt, n_q, n_kv),
        out_shape=jax.ShapeDtypeStruct((bsz, n_q, d), F32),
        grid=(bsz, d // LANES),
        in_specs=[pl.BlockSpec(memory_space=pltpu.SMEM),
                  pl.BlockSpec((1, n_q, LANES), lambda b, p: (b, 0, p)),
                  pl.BlockSpec((1, n_kv, LANES), lambda b, p: (b, 0, p)),
                  pl.BlockSpec((1, n_kv, LANES), lambda b, p: (b, 0, p))],
        out_specs=pl.BlockSpec((1, n_q, LANES), lambda b, p: (b, 0, p)),
        scratch_shapes=[pltpu.VMEM((rows_pad, LANES), BF16), pltpu.VMEM((rows_pad, LANES), BF16),
                        pltpu.VMEM((2, rows_q, ATT_KEYS), F32)],
        compiler_params=_cparams(("parallel", "parallel")),
        name="band_attn",
    )(rel_table, q, k_all, v_all)


def _pad_cols(w, n):
    return jnp.pad(w, ((0, 0), (0, n - w.shape[1])))


def _pad_rows(w, n):
    return jnp.pad(w, ((0, n - w.shape[0]), (0, 0)))


def _round_up(n, m):
    return -(-n // m) * m


def _prep_weights(p):
    row = lambda a: a.reshape(1, -1).astype(F32)
    bf = lambda a: a.astype(BF16)
    heads = jnp.arange(D_MODEL) // HEAD
    sel = (heads[:, None] == jnp.arange(LANES)[None, :]).astype(BF16)
    selt = jnp.concatenate([sel.T, sel.T], axis=0)
    w = dict(sel=sel, selt=selt)
    rw = []
    for l in range(N_A):
        lw_, la_, lg_ = (_round_up(p[n].shape[-1], LANES) for n in ('rw_w1', 'rw_a1', 'rw_g1'))
        consts = [row(p['norm_mix'][l]), p['rw_mix'][l].astype(F32),
                  bf(p['rw_r'][l]), bf(p['rw_k'][l]), bf(p['rw_v'][l]),
                  row(p['rw_w0'][l]), bf(_pad_cols(p['rw_w1'][l], lw_)), bf(_pad_rows(p['rw_w2'][l], lw_)),
                  row(p['rw_a0'][l]), bf(_pad_cols(p['rw_a1'][l], la_)), bf(_pad_rows(p['rw_a2'][l], la_)),
                  bf(_pad_cols(p['rw_g1'][l], lg_)), bf(_pad_rows(p['rw_g2'][l], lg_)),
                  row(p['rw_kk'][l]), row(p['rw_ka'][l]), sel, selt]
        vres = None
        if l > 0:
            lv_ = _round_up(p['rw_v1'].shape[-1], LANES)
            vres = [row(p['rw_v0'][l - 1]), bf(_pad_cols(p['rw_v1'][l - 1], lv_)),
                    bf(_pad_rows(p['rw_v2'][l - 1], lv_))]
        post = [row(p['rw_lnw'][l]), row(p['rw_lnb'][l]), row(p['rw_rk'][l]), bf(p['rw_o'][l]),
                sel, selt]
        rw.append((consts, vres, post))
    w['rw'] = rw
    w['ffn'] = [(row(p['norm_ffn'][l]), bf(p['w_gate'][l]), bf(p['w_up'][l]), bf(p['w_down'][l]))
                for l in range(DEPTH)]
    w['kv'] = (row(p['kv_norm']), bf(p['w_kv']), row(jnp.tile(p['k_norm'], N_HEADS)))
    w['att'] = [(row(p['norm_mix'][N_A + j]), bf(p['wb_q'][j]),
                 row(jnp.tile(p['q_norm'][j], N_HEADS)), p['rel_bias'][j].astype(F32),
                 bf(p['wb_o'][j])) for j in range(N_B)]
    return w


def _run_trunk(x, mod, kvmod, shift0, wkv0, k_hist, v_hist, w, prompt):
    bsz, t, d = x.shape
    sel, selt = w['sel'], w['selt']
    if prompt:
        tm_a, tm, tm_f, tf = min(t, 256), min(t, 512), min(t, 512), 256
        flat = lambda a: a
        unflat = lambda a: a
        modv = lambda m: m[:, None, :]
    else:
        tm_a, tm, tm_f, tf = t, bsz * t, bsz * t, D_FF
        flat = lambda a: a.reshape(1, bsz * t, d)
        unflat = lambda a: a.reshape(bsz, t, d)
        modv = lambda m: jnp.repeat(m, t, axis=0)[None]

    shifts, wkvs = [], []
    v_first = None
    k_new = v_new = k_all = v_all = None
    for l in range(DEPTH):
        sh_m, sc_m, gt_m, sh_f, sc_f, gt_f = jnp.split(mod[l], 6, axis=-1)
        if l < N_A:
            consts, vres, post = w['rw'][l]
            vr = None if vres is None else vres + [v_first]
            r, lw, k, v, kk, b, g, sh_out = _rwkv_pre(
                x, shift0[l][:, None, :], sh_m[:, None, :], sc_m[:, None, :], consts, vr, tm_a)
            if l == 0:
                v_first = v
            shifts.append(sh_out[:, 0, :])
            s0 = wkv0[l].reshape(bsz, d, HEAD)
            tp = _round_up(t, WKV_CHUNK)
            if tp != t:
                padt = lambda a: jnp.pad(a, ((0, 0), (0, tp - t), (0, 0)))
                y, s_t = _wkv(padt(r), padt(lw), padt(k), padt(v), padt(kk), padt(b), s0)
                y = y[:, :t]
            else:
                y, s_t = _wkv(r, lw, k, v, kk, b, s0)
            wkvs.append(s_t.reshape(bsz, N_HEADS, HEAD, HEAD))
            x = unflat(_rwkv_post(flat(y), flat(r), flat(k), flat(v), flat(g), flat(x),
                                  modv(gt_m), post, tm))
        else:
            if l == N_A:
                kv_norm, w_kv, kn = w['kv']
                sh_kv, sc_kv = jnp.split(kvmod, 2, axis=-1)
                k_new, v_new = _attn_kv(flat(x), modv(sh_kv), modv(sc_kv), kv_norm, w_kv, kn,
                                        sel, selt, tm)
                k_new, v_new = unflat(k_new), unflat(v_new)
                if prompt:
                    k_all, v_all = k_new, v_new
                else:
                    k_all = jnp.concatenate([k_hist.reshape(bsz, -1, d), k_new], axis=1)
                    v_all = jnp.concatenate([v_hist.reshape(bsz, -1, d), v_new], axis=1)
            norm_mix, wq, qn, table, wo = w['att'][l - N_A]
            q = unflat(_attn_q(flat(x), modv(sh_m), modv(sc_m), norm_mix, wq, qn, sel, selt, tm))
            o = _band_attn(q, k_all, v_all, table, prompt)
            x = unflat(_attn_out(flat(o), flat(x), modv(gt_m), wo, tm))
        g_f, wg, wu, wd = w['ffn'][l]
        x = unflat(_ffn(flat(x), modv(sh_f), modv(sc_f), modv(gt_f), g_f, wg, wu, wd, tm_f, tf))
    return x, jnp.stack(shifts), jnp.stack(wkvs), k_new, v_new


def kernel(x_prompt, x_sample, c_prompt, c_sample, state_shift, state_wkv, cache_k, cache_v, w_ada, b_ada, norm_mix, norm_ffn, rw_mix, rw_r, rw_k, rw_v, rw_o, rw_w0, rw_w1, rw_w2, rw_a0, rw_a1, rw_a2, rw_v0, rw_v1, rw_v2, rw_g1, rw_g2, rw_kk, rw_ka, rw_rk, rw_lnw, rw_lnb, kv_ada_w, kv_ada_b, kv_norm, w_kv, k_norm, wb_q, q_norm, rel_bias, wb_o, w_gate, w_up, w_down):
    p = dict(norm_mix=norm_mix, norm_ffn=norm_ffn, rw_mix=rw_mix, rw_r=rw_r, rw_k=rw_k, rw_v=rw_v,
             rw_o=rw_o, rw_w0=rw_w0, rw_w1=rw_w1, rw_w2=rw_w2, rw_a0=rw_a0, rw_a1=rw_a1,
             rw_a2=rw_a2, rw_v0=rw_v0, rw_v1=rw_v1, rw_v2=rw_v2, rw_g1=rw_g1, rw_g2=rw_g2,
             rw_kk=rw_kk, rw_ka=rw_ka, rw_rk=rw_rk.reshape(N_A, D_MODEL), rw_lnw=rw_lnw,
             rw_lnb=rw_lnb, kv_norm=kv_norm, w_kv=w_kv, k_norm=k_norm, wb_q=wb_q, q_norm=q_norm,
             rel_bias=rel_bias, wb_o=wb_o, w_gate=w_gate, w_up=w_up, w_down=w_down)
    w = _prep_weights(p)
    b_p, t_p, d = x_prompt.shape
    b_s = x_sample.shape[0]

    c_all = jnp.concatenate([c_prompt, c_sample], axis=0)
    mod = _ada(c_all, w_ada, b_ada)
    kvmod = _ada(c_all, kv_ada_w[None], kv_ada_b[None])[0]

    zeros_shift = jnp.zeros((N_A, b_p, d), F32)
    zeros_wkv = jnp.zeros((N_A, b_p, N_HEADS, HEAD, HEAD), F32)
    y_p, shift_p, wkv_p, k_p, v_p = _run_trunk(
        x_prompt, mod[:, :b_p], kvmod[:b_p], zeros_shift, zeros_wkv, None, None, w, True)
    keep = min(BAND_PAST, t_p)
    k_p = k_p[:, t_p - keep:].reshape(b_p, keep, N_HEADS, HEAD)
    v_p = v_p[:, t_p - keep:].reshape(b_p, keep, N_HEADS, HEAD)

    y_s, shift_s, wkv_s, k_s, v_s = _run_trunk(
        x_sample, mod[:, b_p:], kvmod[b_p:], state_shift, state_wkv, cache_k, cache_v, w, False)
    t_s = x_sample.shape[1]
    k_s = k_s.reshape(b_s, t_s, N_HEADS, HEAD)
    v_s = v_s.reshape(b_s, t_s, N_HEADS, HEAD)
    return (y_p, y_s, shift_p, wkv_p, k_p, v_p, shift_s, wkv_s, k_s, v_s)
```

```python
import functools
import math

import jax
import jax.numpy as jnp
from jax import lax
from jax.experimental import pallas as pl
from jax.experimental.pallas import tpu as pltpu

F32 = jnp.float32
BF16 = jnp.bfloat16

D_MODEL = 1024
DEPTH = 4
N_A = DEPTH // 2
N_B = DEPTH - N_A
HEAD = 64
N_HEADS = D_MODEL // HEAD
CHUNK = 64
N_PAST_CHUNKS = 8
BAND_PAST = N_PAST_CHUNKS * CHUNK
REL_CLIP = 128
D_FF = -(-(8 * D_MODEL) // (3 * 256)) * 256
RMS_EPS = 1e-6
LNX_EPS = 64e-5
NEG_INF = -1e30
LOG2E = math.log2(math.e)

LANES = 128
SUBLANES = 8
MXU_DIM = 256

WKV_CHUNK = 64
WKV_SUB = 4
WKV_GROUP = MXU_DIM
HEADS_PER_GROUP = WKV_GROUP // HEAD
N_GROUPS = D_MODEL // WKV_GROUP
ATT_Q = 2 * CHUNK
ATT_KEYS = BAND_PAST + ATT_Q
BIAS_W = ATT_KEYS + ATT_Q
ATT_PAR = 12
VMEM_LIMIT = 56 * 1024 * 1024


def _cparams(sem):
    return pltpu.CompilerParams(dimension_semantics=sem, vmem_limit_bytes=VMEM_LIMIT)


def _mm(a, b, dims=((1,), (0,))):
    return lax.dot_general(a.astype(BF16), b.astype(BF16), (dims, ((), ())),
                           preferred_element_type=F32)


def _mm_exact(a, b, dims=((1,), (0,))):
    return lax.dot_general(a.astype(F32), b.astype(F32), (dims, ((), ())),
                           precision=lax.Precision.HIGHEST, preferred_element_type=F32)


def _mm_select(a, b01, dims=((1,), (0,))):
    hi = a.astype(BF16)
    r1 = a - hi.astype(F32)
    mid = r1.astype(BF16)
    lo = (r1 - mid.astype(F32)).astype(BF16)
    dn = (dims, ((), ()))
    return (lax.dot_general(hi, b01, dn, preferred_element_type=F32)
            + lax.dot_general(mid, b01, dn, preferred_element_type=F32)
            + lax.dot_general(lo, b01, dn, preferred_element_type=F32))


def _modnorm(x, g, shift, scale):
    ms = jnp.mean(x * x, axis=-1, keepdims=True)
    return x * lax.rsqrt(ms + RMS_EPS) * (g * (1.0 + scale)) + shift


def _sigmoid(z):
    return jax.nn.sigmoid(z)


def _head_sum(z, bd):
    zb = z.astype(BF16)
    return jnp.concatenate([_mm(zb[:, g * MXU_DIM:(g + 1) * MXU_DIM], bd)
                            for g in range(z.shape[1] // MXU_DIM)], axis=1)


def _ada_kernel(c_ref, w_ref, b_ref, o_ref):
    c = c_ref[...]
    s = c * _sigmoid(c)
    o_ref[0] = _mm_exact(s, w_ref[0]) + b_ref[0]


def _ada(c_all, w, b):
    n_l, d, n = w.shape
    bc = c_all.shape[0]
    tn = 1536 if n % 1536 == 0 else n
    return pl.pallas_call(
        _ada_kernel,
        out_shape=jax.ShapeDtypeStruct((n_l, bc, n), F32),
        grid=(n_l, n // tn),
        in_specs=[pl.BlockSpec((bc, d), lambda l, j: (0, 0)),
                  pl.BlockSpec((1, d, tn), lambda l, j: (l, 0, j)),
                  pl.BlockSpec((1, 1, tn), lambda l, j: (l, 0, j))],
        out_specs=pl.BlockSpec((1, bc, tn), lambda l, j: (l, 0, j)),
        compiler_params=_cparams(("parallel", "parallel")),
        name="ada",
    )(c_all, w, b.reshape(n_l, 1, n))


def _tok_spec(tm, d=D_MODEL):
    return pl.BlockSpec((1, tm, d), lambda b, j: (b, j, 0))


def _mod_spec(arr, tm):
    if arr.shape[1] == 1:
        return pl.BlockSpec((1, 1, arr.shape[2]), lambda b, j: (b, 0, 0))
    return pl.BlockSpec((1, tm, arr.shape[2]), lambda b, j: (b, j, 0))


def _const_spec(arr):
    nd = arr.ndim
    return pl.BlockSpec(arr.shape, lambda b, j, _n=nd: (0,) * _n)


def _rwkv_pre_kernel(has_vres, *refs):
    (x_ref, xp_ref, sp_ref, sh_ref, sc_ref, g_ref, mix_ref, wr_ref, wk_ref, wv_ref,
     w0_ref, w1_ref, w2_ref, a0_ref, a1_ref, a2_ref, g1_ref, g2_ref, kkw_ref, kaw_ref,
     bd_ref) = refs[:21]
    refs = refs[21:]
    if has_vres:
        v0_ref, v1_ref, v2_ref, vf_ref = refs[:4]
        refs = refs[4:]
    r_out, lw_out, k_out, v_out, kk_out, b_out, g_out, shift_out = refs

    j = pl.program_id(1)
    rows = [slice(0, x_ref.shape[1])]
    g = g_ref[...]
    sh = sh_ref[0]
    sc = sc_ref[0]
    mix = mix_ref[...]
    hp_row = _modnorm(xp_ref[0][SUBLANES - 1:SUBLANES, :], g, sh, sc)
    prev_row = jnp.where(j == 0, sp_ref[0], hp_row)

    mixed = []
    for rs in rows:
        h = _modnorm(x_ref[0, rs, :], g, sh, sc)
        row = lax.broadcasted_iota(jnp.int32, h.shape, 0)
        h_prev = jnp.where(row == 0, prev_row, pltpu.roll(h, 1, 0))
        prev_row = h[h.shape[0] - 1:, :]
        xx = h_prev - h
        mixed.append([(h + xx * mix[i:i + 1]).astype(BF16) for i in range(6)])
    shift_out[0] = prev_row

    for rs, (xr, xw, xk, xv, xa, xg) in zip(rows, mixed):
        r = _mm(xr, wr_ref[...])
        k = _mm(xk, wk_ref[...])
        v = _mm(xv, wv_ref[...])
        z = w0_ref[...] + _mm(jnp.tanh(_mm(xw, w1_ref[...])), w2_ref[...])
        lw_out[0, rs, :] = -math.exp(-0.5) * _sigmoid(z)
        a_gate = _sigmoid(a0_ref[...] + _mm(_mm(xa, a1_ref[...]), a2_ref[...]))
        g_out[0, rs, :] = _mm(_sigmoid(_mm(xg, g1_ref[...])), g2_ref[...]).astype(BF16)
        if has_vres:
            gate_v = _sigmoid(v0_ref[...] + _mm(_mm(xv, v1_ref[...]), v2_ref[...]))
            v = v + (vf_ref[0, rs, :].astype(F32) - v) * gate_v
        kkr = k * kkw_ref[...]
        ss = _head_sum(kkr * kkr, bd_ref[...])
        kk = kkr * jnp.where(ss >= 1e-24, lax.rsqrt(ss), 1e12)
        r_out[0, rs, :] = r.astype(BF16)
        k_out[0, rs, :] = (k * (1.0 + (a_gate - 1.0) * kaw_ref[...])).astype(BF16)
        v_out[0, rs, :] = v.astype(BF16)
        kk_out[0, rs, :] = kk.astype(BF16)
        b_out[0, rs, :] = (kk * a_gate).astype(BF16)


def _rwkv_pre(x, shift_prev, sh, sc, consts, vres, tm):
    bsz, t, d = x.shape
    has_vres = vres is not None
    nblk8 = tm // SUBLANES
    args = [x, x, shift_prev, sh, sc] + list(consts)
    specs = [_tok_spec(tm),
             pl.BlockSpec((1, SUBLANES, d), lambda b, j: (b, jnp.maximum(j * nblk8 - 1, 0), 0)),
             pl.BlockSpec((1, 1, d), lambda b, j: (b, 0, 0)),
             _mod_spec(sh, tm), _mod_spec(sc, tm)] + [_const_spec(c) for c in consts]
    if has_vres:
        v0, v1, v2, vf = vres
        args += [v0, v1, v2, vf]
        specs += [_const_spec(v0), _const_spec(v1), _const_spec(v2), _tok_spec(tm)]
    tok = lambda dt: jax.ShapeDtypeStruct((bsz, t, d), dt)
    return pl.pallas_call(
        functools.partial(_rwkv_pre_kernel, has_vres),
        out_shape=[tok(BF16), tok(F32)] + [tok(BF16)] * 5 + [jax.ShapeDtypeStruct((bsz, 1, d), F32)],
        grid=(bsz, t // tm),
        in_specs=specs,
        out_specs=[_tok_spec(tm)] * 7 + [pl.BlockSpec((1, 1, d), lambda b, j: (b, 0, 0))],
        compiler_params=_cparams(("parallel", "arbitrary")),
        name="rwkv_pre",
    )(*args)


def _wkv_kernel(n_sub, r_ref, lw_ref, k_ref, v_ref, kk_ref, b_ref, s0_ref, ltri_ref, tile_ref,
                y_ref, sT_ref, s_scr):
    c = pl.program_id(1)
    n_c = pl.num_programs(1)
    L = WKV_CHUNK
    W = WKV_GROUP
    r_i = lax.broadcasted_iota(jnp.int32, (W, W), 0)
    c_i = lax.broadcasted_iota(jnp.int32, (W, W), 1)
    bd = (r_i // HEAD) == (c_i // HEAD)
    t_i = lax.broadcasted_iota(jnp.int32, (L, W), 0)
    s_i = lax.broadcasted_iota(jnp.int32, (L, W), 1) % HEAD
    strict = s_i < t_i
    incl = s_i <= t_i
    eye_row = (s_i == t_i).astype(F32)

    lane_lo = lax.broadcasted_iota(jnp.int32, (L, LANES), 1) < HEAD
    zero_tile = jnp.zeros((L, LANES), BF16)

    def stack(m):
        mb = m.astype(BF16)
        rows = []
        for h in range(HEADS_PER_GROUP):
            t, half = divmod(h, LANES // HEAD)
            keep = lane_lo if half == 0 else ~lane_lo
            blk = jnp.where(keep, mb[:, t * LANES:(t + 1) * LANES], jnp.zeros((), BF16))
            rows.append(jnp.concatenate([blk if tt == t else zero_tile
                                         for tt in range(W // LANES)], axis=1))
        return jnp.concatenate(rows, axis=0)

    @pl.when(c == 0)
    def _():
        for g in range(N_GROUPS):
            s0 = s0_ref[0, g * W:(g + 1) * W, :]
            s_scr[g] = jnp.where(bd, _mm_select(s0, tile_ref[...]), 0.0)

    ltri3 = ltri_ref[...]

    units = [(sub, g) for sub in range(n_sub) for g in range(N_GROUPS)]
    st = {un: {} for un in units}
    for un in units:
        sub, g = un
        q = st[un]
        q['rs'] = slice(sub * L, (sub + 1) * L)
        q['cs'] = slice(g * W, (g + 1) * W)
        lw = lw_ref[0, q['rs'], q['cs']]
        hi = lw.astype(BF16)
        r1 = lw - hi.astype(F32)
        mid = r1.astype(BF16)
        lo = (r1 - mid.astype(F32)).astype(BF16)
        q['lw'] = lw
        q['cum'] = lax.dot_general(ltri3, jnp.concatenate([hi, mid, lo], axis=0),
                                   (((1,), (0,)), ((), ())), preferred_element_type=F32)
    for un in units:
        q = st[un]
        cum = q['cum']
        p = jnp.exp(cum)
        pinv = jnp.exp(-cum)
        q['pend'] = jnp.exp(cum[L - 1:L, :])
        q['at'] = -kk_ref[0, q['rs'], q['cs']] * jnp.exp(cum - q['lw'])
        q['rt'] = r_ref[0, q['rs'], q['cs']] * p
        bt = b_ref[0, q['rs'], q['cs']] * pinv
        kt = k_ref[0, q['rs'], q['cs']] * pinv
        q['bk'] = (jnp.concatenate([bt, kt], axis=0) * q['pend']).astype(BF16)
        q['ar'] = jnp.concatenate([q['at'], q['rt']], axis=0).astype(BF16)
        gram = _mm(q['ar'], jnp.concatenate([stack(bt), stack(kt)], axis=0), ((1,), (1,)))
        q['m_ak'] = jnp.where(strict, gram[:L, W:], 0.0)
        q['q_b'] = jnp.where(incl, gram[L:, :W], 0.0)
        q['q_k'] = jnp.where(incl, gram[L:, W:], 0.0)
        q['pw'] = jnp.where(strict, gram[:L, :W], 0.0)
        q['tinv'] = eye_row + q['pw']
    n_dbl = int(math.log2(L)) - 1
    for un in units:
        q = st[un]
        q['pw'] = _mm(q['pw'], stack(q['pw']))
    for _ in range(n_dbl - 1):
        for un in units:
            q = st[un]
            both = _mm(jnp.concatenate([q['pw'], q['tinv']], axis=0), stack(q['pw']))
            q['pw'] = both[:L]
            q['tinv'] = q['tinv'] + both[L:]
    for un in units:
        q = st[un]
        q['tinv'] = q['tinv'] + _mm(q['tinv'], stack(q['pw']))
    for un in units:
        q = st[un]
        q['v'] = v_ref[0, q['rs'], q['cs']]
        both = _mm(jnp.concatenate([q['m_ak'], q['q_k']], axis=0), stack(q['v']))
        q['xs0'] = both[:L]
        q['y0'] = both[L:]

    s_cur = [s_scr[g] for g in range(N_GROUPS)]
    for sub in range(n_sub):
        qs = [st[(sub, g)] for g in range(N_GROUPS)]
        s_bf = [s.astype(BF16) for s in s_cur]
        xy = [_mm(q['ar'], s, ((1,), (1,))) for q, s in zip(qs, s_bf)]
        xs = [z[:L] + q['xs0'] for q, z in zip(qs, xy)]
        ys = [z[L:] + q['y0'] for q, z in zip(qs, xy)]
        us = [_mm(q['tinv'], stack(x)) for q, x in zip(qs, xs)]
        for g, (q, u, y) in enumerate(zip(qs, us, ys)):
            y_ref[0, q['rs'], q['cs']] = (y + _mm(q['q_b'], stack(u))).astype(BF16)
            upd = _mm(jnp.concatenate([u.astype(BF16), q['v']], axis=0), q['bk'], ((0,), (0,)))
            s_cur[g] = s_cur[g] * q['pend'] + jnp.where(bd, upd, 0.0)
    for g in range(N_GROUPS):
        s_scr[g] = s_cur[g]

    @pl.when(c == n_c - 1)
    def _():
        for g in range(N_GROUPS):
            sT_ref[0, g * W:(g + 1) * W, :] = _mm_select(s_scr[g], tile_ref[...], ((1,), (1,)))


def _wkv(r, lw, k, v, kk, b, s0, n_sub):
    bsz, t, d = r.shape
    L = WKV_CHUNK
    ltri = (jnp.arange(L)[:, None] >= jnp.arange(L)[None, :]).astype(BF16)
    ltri3 = jnp.concatenate([ltri] * 3, axis=1)
    tile = (jnp.arange(HEAD)[:, None] == (jnp.arange(WKV_GROUP)[None, :] % HEAD)).astype(BF16)
    spec = pl.BlockSpec((1, n_sub * L, d), lambda bb, c: (bb, c, 0))
    sspec = pl.BlockSpec((1, d, HEAD), lambda bb, c: (bb, 0, 0))
    return pl.pallas_call(
        functools.partial(_wkv_kernel, n_sub),
        out_shape=[jax.ShapeDtypeStruct((bsz, t, d), BF16),
                   jax.ShapeDtypeStruct((bsz, d, HEAD), F32)],
        grid=(bsz, t // (n_sub * L)),
        in_specs=[spec] * 6 + [sspec, _const_spec(ltri3), _const_spec(tile)],
        out_specs=[spec, sspec],
        scratch_shapes=[pltpu.VMEM((N_GROUPS, WKV_GROUP, WKV_GROUP), F32)],
        compiler_params=_cparams(("parallel", "arbitrary")),
        name="wkv",
    )(r, lw, k, v, kk, b, s0, ltri3, tile)


def _rwkv_mix(y_ref, r_ref, k_ref, v_ref, g_ref, lnw_ref, lnb_ref, rk_ref, bd_ref, rs):
    y = y_ref[0, rs, :].astype(F32)
    bd = bd_ref[...]
    mu = _head_sum(y, bd) * (1.0 / HEAD)
    dlt = y - mu
    var = _head_sum(dlt * dlt, bd) * (1.0 / HEAD)
    yn = dlt * lax.rsqrt(var + LNX_EPS) * lnw_ref[...] + lnb_ref[...]
    rk = r_ref[0, rs, :].astype(F32) * k_ref[0, rs, :].astype(F32) * rk_ref[...]
    bonus = _head_sum(rk, bd) * v_ref[0, rs, :].astype(F32)
    return (yn + bonus) * g_ref[0, rs, :].astype(F32)


def _mod_rows(ref, rs):
    return ref[0] if ref.shape[1] == 1 else ref[0, rs, :]


def _mix_ffn_kernel(rwkv, n_split, tf, *refs):
    n_mix = 9 if rwkv else 1
    mix_refs, refs = refs[:n_mix], refs[n_mix:]
    (x_ref, gm_ref, wo_ref, sh_ref, sc_ref, gate_ref, g_ref, wg_ref, wu_ref, wd_ref,
     o_ref, z_scr) = refs
    tm = x_ref.shape[1]
    rows = [slice(i * (tm // n_split), (i + 1) * (tm // n_split)) for i in range(n_split)]
    hs = []
    for rs in rows:
        mixed = _rwkv_mix(*mix_refs, rs) if rwkv else mix_refs[0][0, rs, :]
        x1 = x_ref[0, rs, :] + _mod_rows(gm_ref, rs) * _mm(mixed, wo_ref[...])
        o_ref[0, rs, :] = x1
        hs.append(_modnorm(x1, g_ref[...], _mod_rows(sh_ref, rs), _mod_rows(sc_ref, rs)).astype(BF16))
    for f in range(wg_ref.shape[1] // tf):
        fs = slice(f * tf, (f + 1) * tf)
        for rs, h in zip(rows, hs):
            a = _mm(h, wg_ref[:, fs])
            u = _mm(h, wu_ref[:, fs])
            z_scr[rs, fs] = (a * _sigmoid(a) * u).astype(BF16)
    for rs in rows:
        o_ref[0, rs, :] += _mod_rows(gate_ref, rs) * _mm(z_scr[rs, :], wd_ref[...])


def _mix_ffn(mix_toks, mix_consts, x, gm, wo, sh, sc, gate, g, wg, wu, wd, tm, tf):
    bsz, t, d = x.shape
    dff = wg.shape[1]
    rwkv = len(mix_toks) > 1
    n_split = 2 if tm % (4 * SUBLANES) == 0 and tm >= MXU_DIM else 1
    resident = lambda a: pl.BlockSpec(a.shape, lambda b, j: (0, 0), pipeline_mode=pl.Buffered(1))
    return pl.pallas_call(
        functools.partial(_mix_ffn_kernel, rwkv, n_split, tf),
        out_shape=jax.ShapeDtypeStruct((bsz, t, d), F32),
        grid=(bsz, t // tm),
        in_specs=[_tok_spec(tm)] * len(mix_toks) + [_const_spec(c) for c in mix_consts]
        + [_tok_spec(tm), _mod_spec(gm, tm), resident(wo), _mod_spec(sh, tm), _mod_spec(sc, tm),
           _mod_spec(gate, tm), _const_spec(g), resident(wg), resident(wu), resident(wd)],
        out_specs=_tok_spec(tm),
        scratch_shapes=[pltpu.VMEM((tm, dff), BF16)],
        compiler_params=_cparams(("parallel", "parallel")),
        name="mix_ffn",
    )(*mix_toks, *mix_consts, x, gm, wo, sh, sc, gate, g, wg, wu, wd)


def _head_rms(z, w_tiled, bd):
    ms = _head_sum(z * z, bd) * (1.0 / HEAD)
    return z * lax.rsqrt(ms + RMS_EPS) * w_tiled


def _attn_q_kernel(x_ref, sh_ref, sc_ref, g_ref, wq_ref, qn_ref, bd_ref, o_ref):
    h = _modnorm(x_ref[0], g_ref[...], sh_ref[0], sc_ref[0])
    o_ref[0] = _head_rms(_mm(h, wq_ref[...]), qn_ref[...], bd_ref[...]).astype(BF16)


def _attn_q(x, sh, sc, g, wq, qn, bd, tm):
    bsz, t, d = x.shape
    consts = [g, wq, qn, bd]
    return pl.pallas_call(
        _attn_q_kernel,
        out_shape=jax.ShapeDtypeStruct((bsz, t, d), BF16),
        grid=(bsz, t // tm),
        in_specs=[_tok_spec(tm), _mod_spec(sh, tm), _mod_spec(sc, tm)]
        + [_const_spec(c) for c in consts],
        out_specs=_tok_spec(tm),
        compiler_params=_cparams(("parallel", "parallel")),
        name="attn_q",
    )(x, sh, sc, *consts)


def _attn_qkv_kernel(x_ref, shk_ref, sck_ref, shq_ref, scq_ref, gk_ref, wkv_ref, kn_ref,
                     gq_ref, wq_ref, qn_ref, bd_ref, k_ref, v_ref, kt_ref, vt_ref, q_ref):
    x = x_ref[0]
    kv = _mm(_modnorm(x, gk_ref[...], shk_ref[0], sck_ref[0]), wkv_ref[...])
    k = _head_rms(kv[:, :D_MODEL], kn_ref[...], bd_ref[...])
    v = kv[:, D_MODEL:]
    k_ref[0] = k.astype(BF16)
    v_ref[0] = v.astype(BF16)
    kt_ref[0] = k
    vt_ref[0] = v
    hq = _modnorm(x, gq_ref[...], shq_ref[0], scq_ref[0])
    q_ref[0] = _head_rms(_mm(hq, wq_ref[...]), qn_ref[...], bd_ref[...]).astype(BF16)


def _attn_qkv(x, shk, sck, shq, scq, gk, wkv, kn, gq, wq, qn, bd, tm, keep):
    bsz, t, d = x.shape
    assert keep % tm == 0 and (t - keep) % tm == 0, (t, keep, tm)
    first_tail = (t - keep) // tm
    mods = [shk, sck, shq, scq]
    consts = [gk, wkv, kn, gq, wq, qn, bd]
    tail_spec = pl.BlockSpec((1, tm, d), lambda b, j: (b, jnp.maximum(j - first_tail, 0), 0))
    tok = jax.ShapeDtypeStruct((bsz, t, d), BF16)
    tail = jax.ShapeDtypeStruct((bsz, keep, d), F32)
    return pl.pallas_call(
        _attn_qkv_kernel,
        out_shape=[tok, tok, tail, tail, tok],
        grid=(bsz, t // tm),
        in_specs=[_tok_spec(tm)] + [_mod_spec(m, tm) for m in mods]
        + [_const_spec(c) for c in consts],
        out_specs=[_tok_spec(tm), _tok_spec(tm), tail_spec, tail_spec, _tok_spec(tm)],
        compiler_params=_cparams(("parallel", "arbitrary")),
        name="attn_qkv",
    )(x, *mods, *consts)


def _rel_bias_kernel(prompt, rows_q, n_kv, tab_ref, pick_ref, o_ref):
    base = _mm_select(tab_ref[...], pick_ref[...])
    qi = lax.broadcasted_iota(jnp.int32, (rows_q, ATT_KEYS), 0)
    ji = lax.broadcasted_iota(jnp.int32, (rows_q, ATT_KEYS), 1)
    if prompt:
        kc = ji // CHUNK - N_PAST_CHUNKS
        qc = qi // CHUNK
        ok = (kc <= qc) & (kc >= qc - N_PAST_CHUNKS)
    else:
        ok = ji < n_kv
    for h in range(N_HEADS):
        full = jnp.broadcast_to(base[h:h + 1, :], (rows_q, BIAS_W))
        rolled = pltpu.roll(full, BIAS_W - ATT_Q, 1, stride=1, stride_axis=0)
        masked = jnp.where(ok, rolled[:, :ATT_KEYS], NEG_INF)
        if prompt:
            o_ref[h // 2, :, (h % 2) * ATT_Q:(h % 2 + 1) * ATT_Q] = (masked * LOG2E).T
        else:
            o_ref[h] = masked


def _rel_bias(rel_table, prompt, rows_q, n_kv):
    n_tab = rel_table.shape[1]
    n_pad = _round_up(n_tab, LANES)
    tab = jnp.pad(rel_table.astype(F32), ((0, 0), (0, n_pad - n_tab)))
    want = jnp.minimum(BIAS_W - jnp.arange(BIAS_W), 2 * REL_CLIP)
    pick = (jnp.arange(n_pad)[:, None] == want[None, :]).astype(BF16)
    shape = (N_HEADS // 2, ATT_KEYS, 2 * ATT_Q) if prompt else (N_HEADS, rows_q, ATT_KEYS)
    return pl.pallas_call(
        functools.partial(_rel_bias_kernel, prompt, rows_q, n_kv),
        out_shape=jax.ShapeDtypeStruct(shape, F32),
        name="rel_bias",
    )(tab, pick)


def _band_attn_prompt_kernel(n_q, bias_ref, q_ref, k_ref, v_ref, o_ref, kpad, vt3):
    n_blk = ATT_KEYS // ATT_Q
    kpad[:BAND_PAST, :] = jnp.zeros((BAND_PAST, LANES), BF16)
    kpad[BAND_PAST:, :] = k_ref[0]
    for i in range(BAND_PAST // ATT_Q):
        vt3[i] = jnp.zeros((LANES, ATT_Q), BF16)
    for i in range(n_q // ATT_Q):
        vt3[BAND_PAST // ATT_Q + i] = (
            v_ref[0, i * ATT_Q:(i + 1) * ATT_Q, :].astype(F32).T.astype(BF16))

    row = lax.broadcasted_iota(jnp.int32, (LANES, ATT_Q), 0)
    top = row < HEAD
    jrow = lax.broadcasted_iota(jnp.int32, (ATT_KEYS, 2 * ATT_Q), 0)
    scale = HEAD ** -0.5 * LOG2E

    def make_step(masked, par, first):
        def step(it, carry):
            ss = [first + it * par + u for u in range(par)]
            q0s = [pl.multiple_of(s * ATT_Q, ATT_Q) for s in ss]
            sts = []
            for q0 in q0s:
                q2t = (q_ref[0, pl.ds(q0, ATT_Q), :].astype(F32) * scale).T
                qbd = jnp.concatenate([jnp.where(top, q2t, 0.0), jnp.where(top, 0.0, q2t)], axis=1)
                st = _mm(kpad[pl.ds(q0, ATT_KEYS), :], qbd) + bias_ref[0]
                if masked:
                    st = jnp.where(jrow >= BAND_PAST - q0, st, NEG_INF)
                sts.append(st)
            ps = [jnp.exp2(st - jnp.max(st, axis=0, keepdims=True)) for st in sts]
            ots = []
            for s, p in zip(ss, ps):
                l = jnp.sum(p, axis=0, keepdims=True)
                vtb = jnp.concatenate([vt3[s + c] for c in range(n_blk)], axis=1)
                ots.append(_mm(vtb, p) * (1.0 / l))
            for q0, ot in zip(q0s, ots):
                o_ref[0, pl.ds(q0, ATT_Q), :] = (
                    jnp.where(top, ot[:, :ATT_Q], ot[:, ATT_Q:]).T.astype(BF16))
            return carry
        return step

    n_steps = n_q // ATT_Q
    n_masked = min(BAND_PAST // ATT_Q, n_steps)
    n_rest = n_steps - n_masked
    pick = lambda n: max(p for p in range(1, ATT_PAR + 1) if n % p == 0)
    par_m = pick(n_masked)
    lax.fori_loop(0, n_masked // par_m, make_step(True, par_m, 0), 0)
    if n_rest:
        par_r = pick(n_rest)
        lax.fori_loop(0, n_rest // par_r, make_step(False, par_r, n_masked), 0)


def _band_attn_sample_kernel(bias_ref, q_ref, k_ref, v_ref, o_ref):
    n_q = q_ref.shape[1]
    lane = lax.broadcasted_iota(jnp.int32, (n_q, LANES), 1)
    lo = lane < HEAD
    n_pairs = q_ref.shape[2] // LANES
    cols = [slice(pair * LANES, (pair + 1) * LANES) for pair in range(n_pairs)]
    scores = []
    for pair, cs in enumerate(cols):
        q2 = q_ref[0, :, cs].astype(F32) * HEAD ** -0.5
        for hh in range(2):
            qh = jnp.where(lo if hh == 0 else ~lo, q2, 0.0)
            scores.append(_mm(qh, k_ref[0, :, cs], ((1,), (1,))) + bias_ref[2 * pair + hh])
    probs = [jnp.exp(sc - jnp.max(sc, axis=-1, keepdims=True)) for sc in scores]
    outs = [_mm(p, v_ref[0, :, cols[i // 2]]) / jnp.sum(p, axis=-1, keepdims=True)
            for i, p in enumerate(probs)]
    for pair, cs in enumerate(cols):
        o_ref[0, :, cs] = jnp.where(lo, outs[2 * pair], outs[2 * pair + 1]).astype(BF16)


def _band_attn(q, k_all, v_all, rel_table, prompt, n_kv=None):
    bsz, n_q, d = q.shape
    out_shape = jax.ShapeDtypeStruct((bsz, n_q, d), BF16)
    if not prompt:
        bias = _rel_bias(rel_table, False, n_q, n_kv)
        whole = lambda n: pl.BlockSpec((1, n, d), lambda b: (b, 0, 0))
        return pl.pallas_call(
            _band_attn_sample_kernel,
            out_shape=out_shape,
            grid=(bsz,),
            in_specs=[pl.BlockSpec(bias.shape, lambda b: (0, 0, 0)),
                      whole(n_q), whole(ATT_KEYS), whole(ATT_KEYS)],
            out_specs=whole(n_q),
            compiler_params=_cparams(("parallel",)),
            name="band_attn_sample",
        )(bias, q, k_all, v_all)
    tok = pl.BlockSpec((1, n_q, LANES), lambda b, p: (b, 0, p))
    bias = _rel_bias(rel_table, True, ATT_Q, n_q)
    return pl.pallas_call(
        functools.partial(_band_attn_prompt_kernel, n_q),
        out_shape=out_shape,
        grid=(bsz, d // LANES),
        in_specs=[pl.BlockSpec((1, ATT_KEYS, 2 * ATT_Q), lambda b, p: (p, 0, 0)), tok, tok, tok],
        out_specs=tok,
        scratch_shapes=[pltpu.VMEM((BAND_PAST + n_q, LANES), BF16),
                        pltpu.VMEM(((BAND_PAST + n_q) // ATT_Q, LANES, ATT_Q), BF16)],
        compiler_params=_cparams(("parallel", "parallel")),
        name="band_attn",
    )(bias, q, k_all, v_all)


def _pad_cols(w, n):
    return jnp.pad(w, ((0, 0), (0, n - w.shape[1])))


def _pad_rows(w, n):
    return jnp.pad(w, ((0, n - w.shape[0]), (0, 0)))


def _round_up(n, m):
    return -(-n // m) * m


def _prep_weights(p):
    row = lambda a: a.reshape(1, -1).astype(F32)
    bf = lambda a: a.astype(BF16)
    heads = jnp.arange(MXU_DIM) // HEAD
    bd = (heads[:, None] == heads[None, :]).astype(BF16)
    w = dict(bd=bd)
    rw = []
    for l in range(N_A):
        lw_, la_, lg_ = (_round_up(p[n].shape[-1], LANES) for n in ('rw_w1', 'rw_a1', 'rw_g1'))
        consts = [row(p['norm_mix'][l]), p['rw_mix'][l].astype(F32),
                  bf(p['rw_r'][l]), bf(p['rw_k'][l]), bf(p['rw_v'][l]),
                  row(p['rw_w0'][l]), bf(_pad_cols(p['rw_w1'][l], lw_)), bf(_pad_rows(p['rw_w2'][l], lw_)),
                  row(p['rw_a0'][l]), bf(_pad_cols(p['rw_a1'][l], la_)), bf(_pad_rows(p['rw_a2'][l], la_)),
                  bf(_pad_cols(p['rw_g1'][l], lg_)), bf(_pad_rows(p['rw_g2'][l], lg_)),
                  row(p['rw_kk'][l]), row(p['rw_ka'][l]), bd]
        vres = None
        if l > 0:
            lv_ = _round_up(p['rw_v1'].shape[-1], LANES)
            vres = [row(p['rw_v0'][l - 1]), bf(_pad_cols(p['rw_v1'][l - 1], lv_)),
                    bf(_pad_rows(p['rw_v2'][l - 1], lv_))]
        post = [row(p['rw_lnw'][l]), row(p['rw_lnb'][l]), row(p['rw_rk'][l]), bf(p['rw_o'][l]), bd]
        rw.append((consts, vres, post))
    w['rw'] = rw
    w['ffn'] = [(row(p['norm_ffn'][l]), bf(p['w_gate'][l]), bf(p['w_up'][l]), bf(p['w_down'][l]))
                for l in range(DEPTH)]
    w['kv'] = (row(p['kv_norm']), bf(p['w_kv']), row(jnp.tile(p['k_norm'], N_HEADS)))
    w['att'] = [(row(p['norm_mix'][N_A + j]), bf(p['wb_q'][j]),
                 row(jnp.tile(p['q_norm'][j], N_HEADS)), p['rel_bias'][j].astype(F32),
                 bf(p['wb_o'][j])) for j in range(N_B)]
    return w


def _run_trunk(x, mod, kvmod, shift0, wkv0, k_hist, v_hist, w, prompt):
    bsz, t, d = x.shape
    bd = w['bd']
    if prompt:
        tm_a, tm, tm_f, tf = min(t, 512), min(t, 512), min(t, 512), MXU_DIM
        assert t % tm_a == 0 and t % tm == 0 and t % tm_f == 0 and t % ATT_Q == 0, t
        flat = lambda a: a
        unflat = lambda a: a
        modv = lambda m: m[:, None, :]
    else:
        tm_a, tm, tm_f, tf = t, bsz * t, bsz * t, MXU_DIM
        flat = lambda a: a.reshape(1, bsz * t, d)
        unflat = lambda a: a.reshape(bsz, t, d)
        modv = lambda m: jnp.repeat(m, t, axis=0)[None]

    shifts, wkvs = [], []
    v_first = None
    k_new = v_new = k_all = v_all = None
    for l in range(DEPTH):
        sh_m, sc_m, gt_m, sh_f, sc_f, gt_f = jnp.split(mod[l], 6, axis=-1)
        if l < N_A:
            consts, vres, post = w['rw'][l]
            vr = None if vres is None else vres + [v_first]
            r, lw, k, v, kk, b, g, sh_out = _rwkv_pre(
                x, shift0[l][:, None, :], sh_m[:, None, :], sc_m[:, None, :], consts, vr, tm_a)
            if l == 0:
                v_first = v
            shifts.append(sh_out[:, 0, :])
            s0 = wkv0[l].reshape(bsz, d, HEAD)
            tp = _round_up(t, WKV_CHUNK)
            n_sub = WKV_SUB if tp % (WKV_SUB * WKV_CHUNK) == 0 else 1
            if tp != t:
                padt = lambda a: jnp.pad(a, ((0, 0), (0, tp - t), (0, 0)))
                y, s_t = _wkv(padt(r), padt(lw), padt(k), padt(v), padt(kk), padt(b), s0, n_sub)
                y = y[:, :t]
            else:
                y, s_t = _wkv(r, lw, k, v, kk, b, s0, n_sub)
            wkvs.append(s_t.reshape(bsz, N_HEADS, HEAD, HEAD))
            lnw, lnb, rk, wo, _ = post
            mix_toks = [flat(a) for a in (y, r, k, v, g)]
            mix_consts = [lnw, lnb, rk, bd]
        else:
            norm_mix, wq, qn, table, wo = w['att'][l - N_A]
            if l == N_A:
                kv_norm, w_kv, kn = w['kv']
                sh_kv, sc_kv = jnp.split(kvmod, 2, axis=-1)
                keep = tm if not prompt else min(BAND_PAST, t)
                k_bf, v_bf, k_new, v_new, q = _attn_qkv(
                    flat(x), modv(sh_kv), modv(sc_kv), modv(sh_m), modv(sc_m), kv_norm, w_kv, kn,
                    norm_mix, wq, qn, bd, tm, keep)
                if prompt:
                    k_all, v_all, n_kv = k_bf, v_bf, None
                else:
                    k_new, v_new = unflat(k_new), unflat(v_new)
                    n_kv = k_hist.shape[1] + t
                    assert k_hist.shape[1] == BAND_PAST and n_kv <= ATT_KEYS, k_hist.shape
                    zpad = jnp.zeros((bsz, ATT_KEYS - n_kv, d), BF16)
                    hist = lambda c: c.reshape(bsz, -1, d).astype(BF16)
                    k_all = jnp.concatenate([hist(k_hist), unflat(k_bf), zpad], axis=1)
                    v_all = jnp.concatenate([hist(v_hist), unflat(v_bf), zpad], axis=1)
            else:
                q = _attn_q(flat(x), modv(sh_m), modv(sc_m), norm_mix, wq, qn, bd, tm)
            mix_toks = [flat(_band_attn(unflat(q), k_all, v_all, table, prompt, n_kv))]
            mix_consts = []
        g_f, wg, wu, wd = w['ffn'][l]
        x = unflat(_mix_ffn(mix_toks, mix_consts, flat(x), modv(gt_m), wo, modv(sh_f), modv(sc_f),
                            modv(gt_f), g_f, wg, wu, wd, tm_f, tf))
    return x, jnp.stack(shifts), jnp.stack(wkvs), k_new, v_new


def kernel(x_prompt, x_sample, c_prompt, c_sample, state_shift, state_wkv, cache_k, cache_v, w_ada, b_ada, norm_mix, norm_ffn, rw_mix, rw_r, rw_k, rw_v, rw_o, rw_w0, rw_w1, rw_w2, rw_a0, rw_a1, rw_a2, rw_v0, rw_v1, rw_v2, rw_g1, rw_g2, rw_kk, rw_ka, rw_rk, rw_lnw, rw_lnb, kv_ada_w, kv_ada_b, kv_norm, w_kv, k_norm, wb_q, q_norm, rel_bias, wb_o, w_gate, w_up, w_down):
    p = dict(norm_mix=norm_mix, norm_ffn=norm_ffn, rw_mix=rw_mix, rw_r=rw_r, rw_k=rw_k, rw_v=rw_v,
             rw_o=rw_o, rw_w0=rw_w0, rw_w1=rw_w1, rw_w2=rw_w2, rw_a0=rw_a0, rw_a1=rw_a1,
             rw_a2=rw_a2, rw_v0=rw_v0, rw_v1=rw_v1, rw_v2=rw_v2, rw_g1=rw_g1, rw_g2=rw_g2,
             rw_kk=rw_kk, rw_ka=rw_ka, rw_rk=rw_rk.reshape(N_A, D_MODEL), rw_lnw=rw_lnw,
             rw_lnb=rw_lnb, kv_norm=kv_norm, w_kv=w_kv, k_norm=k_norm, wb_q=wb_q, q_norm=q_norm,
             rel_bias=rel_bias, wb_o=wb_o, w_gate=w_gate, w_up=w_up, w_down=w_down)
    w = _prep_weights(p)
    b_p, t_p, d = x_prompt.shape
    b_s = x_sample.shape[0]

    c_all = jnp.concatenate([c_prompt, c_sample], axis=0)
    mod = _ada(c_all, w_ada, b_ada)
    kvmod = _ada(c_all, kv_ada_w[None], kv_ada_b[None])[0]

    zeros_shift = jnp.zeros((N_A, b_p, d), F32)
    zeros_wkv = jnp.zeros((N_A, b_p, N_HEADS, HEAD, HEAD), F32)
    y_p, shift_p, wkv_p, k_p, v_p = _run_trunk(
        x_prompt, mod[:, :b_p], kvmod[:b_p], zeros_shift, zeros_wkv, None, None, w, True)
    keep = min(BAND_PAST, t_p)
    k_p = k_p.reshape(b_p, keep, N_HEADS, HEAD)
    v_p = v_p.reshape(b_p, keep, N_HEADS, HEAD)

    y_s, shift_s, wkv_s, k_s, v_s = _run_trunk(
        x_sample, mod[:, b_p:], kvmod[b_p:], state_shift, state_wkv, cache_k, cache_v, w, False)
    t_s = x_sample.shape[1]
    k_s = k_s.reshape(b_s, t_s, N_HEADS, HEAD)
    v_s = v_s.reshape(b_s, t_s, N_HEADS, HEAD)
    return (y_p, y_s, shift_p, wkv_p, k_p, v_p, shift_s, wkv_s, k_s, v_s)
```
